```python
import math
import jax, jax.numpy as jnp
from jax import lax
import numpy as np

D_MODEL = 1024
BATCH = 32
SEQ = 2048
DEPTH = 1
DEC_BATCH = 8
DEC_SEQ = 2048
PAST_LEN = 128

ATTN_WIDTH = D_MODEL // 2
HYENA_WIDTH = D_MODEL - ATTN_WIDTH
HEAD_DIM = 64
N_HEADS = ATTN_WIDTH // HEAD_DIM
DILATED_BRANCHES = ((128, 1), (512, 4), (2048, 16))
ATTN_BLOCK = 64
ROPE_THETA = 10000.0
HYENA_ORDER = 2
SHORT_CONV = 3
FILTER_EMB = 33
FILTER_HIDDEN = 64
FILTER_OUT = HYENA_ORDER * 2 * HYENA_WIDTH
DECAY_TARGET = 1e-2
FAST_DECAY_PCT = 0.3
SLOW_DECAY_PCT = 1.5
D_FF = 4 * D_MODEL
IN_WIDTH = 3 * ATTN_WIDTH + 3 * HYENA_WIDTH
EPS = 1e-6
NEG_INF = -1e30

kernel_name = "hybrid_dilated_attn_hyena_encoder"


def rmsnorm(x, g):
    xf = x.astype(jnp.float32)
    y = xf * lax.rsqrt(jnp.mean(xf * xf, axis=-1, keepdims=True) + EPS) * g.astype(jnp.float32)
    return y.astype(x.dtype)


def rmsnorm_f32(x, g):
    return x * lax.rsqrt(jnp.mean(x * x, axis=-1, keepdims=True) + EPS) * g.astype(jnp.float32)


def rope(x):
    S, E = x.shape[1], x.shape[3]
    half = E // 2
    inv_freq = ROPE_THETA ** (-jnp.arange(half, dtype=jnp.float32) / half)
    ang = jnp.arange(S, dtype=jnp.float32)[:, None] * inv_freq[None, :]
    c = jnp.cos(ang)[None, :, None, :]
    s = jnp.sin(ang)[None, :, None, :]
    x1, x2 = x[..., :half], x[..., half:]
    return jnp.concatenate([x1 * c - x2 * s, x2 * c + x1 * s], axis=-1)


def window_attention(q, k, v, half):
    N, L, H, E = q.shape
    blk = ATTN_BLOCK
    nb = -(-L // blk)
    Lp = nb * blk
    W = blk + 2 * half
    qb = jnp.pad(q, ((0, 0), (0, Lp - L), (0, 0), (0, 0))).reshape(N, nb, blk, H, E)
    pad_kv = ((0, 0), (half, Lp - L + half), (0, 0), (0, 0))
    kp = jnp.pad(k, pad_kv)
    vp = jnp.pad(v, pad_kv)
    idx = (jnp.arange(nb) * blk)[:, None] + jnp.arange(W)[None, :]
    kb = kp[:, idx]
    vb = vp[:, idx]
    s = jnp.einsum('nbqhe,nbkhe->nbhqk', qb, kb) * (1.0 / math.sqrt(E))
    qpos = (jnp.arange(nb) * blk)[:, None] + jnp.arange(blk)[None, :]
    kpos = idx - half
    rel = kpos[:, None, :] - qpos[:, :, None]
    valid = (jnp.abs(rel) <= half) & (kpos >= 0)[:, None, :] & (kpos < L)[:, None, :]
    s = jnp.where(valid[None, :, None, :, :], s, NEG_INF)
    lse = jax.nn.logsumexp(s, axis=-1)
    p = jnp.exp(s - lse[..., None])
    out = jnp.einsum('nbhqk,nbkhe->nbqhe', p, vb).reshape(N, Lp, H, E)[:, :L]
    lse = lse.transpose(0, 1, 3, 2).reshape(N, Lp, H)[:, :L]
    return out, lse


def dilated_branch(q, k, v, window, dilation):
    B, S, H, E = q.shape
    L = S // dilation
    half = window // (2 * dilation)

    def to_res(t):
        return t.reshape(B, L, dilation, H, E).transpose(0, 2, 1, 3, 4).reshape(B * dilation, L, H, E)

    out, lse = window_attention(to_res(q), to_res(k), to_res(v), half)
    out = out.reshape(B, dilation, L, H, E).transpose(0, 2, 1, 3, 4).reshape(B, S, H, E)
    lse = lse.reshape(B, dilation, L, H).transpose(0, 2, 1, 3).reshape(B, S, H)
    return out, lse


def dilated_attention(q, k, v):
    outs, lses = [], []
    for window, dilation in DILATED_BRANCHES:
        o, l = dilated_branch(q, k, v, window, dilation)
        outs.append(o)
        lses.append(l)
    wts = jax.nn.softmax(jnp.stack(lses, axis=0), axis=0)
    return jnp.sum(wts[..., None] * jnp.stack(outs, axis=0), axis=0)


def short_conv(u, w, b):
    pad = SHORT_CONV // 2
    S = u.shape[1]
    up = jnp.pad(u, ((0, 0), (pad, SHORT_CONV - 1 - pad), (0, 0)))
    y = b
    for t in range(SHORT_CONV):
        y = y + up[:, t:t + S] * w[t]
    return y


def implicit_filter_spectra(L, w1, b1, f1, w2, b2, f2, w3, b3):
    pos = jnp.arange(L, dtype=jnp.float32)
    t = pos / max(L - 1, 1)
    bands = (FILTER_EMB - 1) // 2
    fr = jnp.linspace(1e-4, bands - 1, bands, dtype=jnp.float32)
    ang = 2.0 * math.pi * pos[:, None] * fr[None, :] / L
    feat = jnp.concatenate([t[:, None], jnp.cos(ang), -jnp.sin(ang)], axis=-1)
    h = jnp.sin(f1 * (feat @ w1 + b1))
    h = jnp.sin(f2 * (h @ w2 + b2))
    h = (h @ w3 + b3).reshape(L, HYENA_ORDER, 2, HYENA_WIDTH)
    deltas = jnp.linspace(math.log(DECAY_TARGET) / FAST_DECAY_PCT,
                          math.log(DECAY_TARGET) / SLOW_DECAY_PCT, HYENA_WIDTH, dtype=jnp.float32)
    decay = jnp.exp(-t[:, None] * jnp.abs(deltas)[None, :])
    h = h * decay[:, None, None, :]
    spec = jnp.fft.rfft(h, n=2 * L, axis=0)
    return spec[:, :, 0] + jnp.conj(spec[:, :, 1])


def long_conv(z, Hn):
    L = z.shape[1]
    Z = jnp.fft.rfft(z, n=2 * L, axis=1)
    return jnp.fft.irfft(Z * Hn[None], n=2 * L, axis=1)[:, :L]


def hyena_mixer(u, conv_w, conv_b, skip, spectra):
    u = short_conv(u, conv_w, conv_b)
    z, x1, x2 = jnp.split(u, 3, axis=-1)
    gates = (x1, x2)
    for n in range(HYENA_ORDER):
        z = gates[n] * (long_conv(z, spectra[:, n]) + skip[n] * z)
    return z


def encoder_layer(x, mix_norm, w_in, q_norm, k_norm, hy_conv_w, hy_conv_b,
                  flt_w1, flt_b1, flt_freq1, flt_w2, flt_b2, flt_freq2, flt_w3, flt_b3,
                  hy_skip, attn_out_norm, hy_out_norm, w_out, ffn_norm, w_up, w_down):
    B, S, _ = x.shape
    f32 = jnp.float32
    xn = rmsnorm(x, mix_norm)
    proj = xn @ w_in
    a = ATTN_WIDTH
    q = proj[..., 0:a].astype(f32).reshape(B, S, N_HEADS, HEAD_DIM)
    k = proj[..., a:2 * a].astype(f32).reshape(B, S, N_HEADS, HEAD_DIM)
    v = proj[..., 2 * a:3 * a].astype(f32).reshape(B, S, N_HEADS, HEAD_DIM)
    u = proj[..., 3 * a:].astype(f32)
    q = rope(rmsnorm_f32(q, q_norm))
    k = rope(rmsnorm_f32(k, k_norm))
    attn = dilated_attention(q, k, v).reshape(B, S, ATTN_WIDTH)

    spectra = implicit_filter_spectra(S, flt_w1.astype(f32), flt_b1.astype(f32), flt_freq1.astype(f32),
                                      flt_w2.astype(f32), flt_b2.astype(f32), flt_freq2.astype(f32),
                                      flt_w3.astype(f32), flt_b3.astype(f32))
    hy = hyena_mixer(u, hy_conv_w.astype(f32), hy_conv_b.astype(f32), hy_skip.astype(f32), spectra)

    mixed = jnp.concatenate([rmsnorm_f32(attn, attn_out_norm), rmsnorm_f32(hy, hy_out_norm)], axis=-1)
    h = x + mixed.astype(x.dtype) @ w_out

    hn = rmsnorm(h, ffn_norm)
    ff = jnp.square(jax.nn.relu(hn @ w_up))
    return h + ff @ w_down


def trunk(x, mix_norm, w_in, q_norm, k_norm, hy_conv_w, hy_conv_b,
          flt_w1, flt_b1, flt_freq1, flt_w2, flt_b2, flt_freq2, flt_w3, flt_b3,
          hy_skip, attn_out_norm, hy_out_norm, w_out, ffn_norm, w_up, w_down):
    for i in range(DEPTH):
        x = encoder_layer(x, mix_norm[i], w_in[i], q_norm[i], k_norm[i], hy_conv_w[i], hy_conv_b[i],
                          flt_w1[i], flt_b1[i], flt_freq1[i], flt_w2[i], flt_b2[i], flt_freq2[i],
                          flt_w3[i], flt_b3[i], hy_skip[i], attn_out_norm[i], hy_out_norm[i],
                          w_out[i], ffn_norm[i], w_up[i], w_down[i])
    return x


def setup_inputs(seed: int = 0) -> dict:
    key = jax.random.key(seed)
    ks = jax.random.split(key, 24)
    f32 = jnp.float32

    def nrm(k, shape, scale):
        return jax.random.normal(k, shape, f32) * scale

    def gain(k, shape):
        return 1.0 + 0.02 * jax.random.normal(k, shape, f32)

    C = HYENA_WIDTH
    return {
        "x_prompt": nrm(ks[0], (BATCH, SEQ, D_MODEL), 1.0),
        "x_sample": nrm(ks[1], (DEC_BATCH, DEC_SEQ, D_MODEL), 1.0),
        "mix_norm": gain(ks[2], (DEPTH, D_MODEL)),
        "w_in": nrm(ks[3], (DEPTH, D_MODEL, IN_WIDTH), D_MODEL ** -0.5),
        "q_norm": gain(ks[4], (DEPTH, HEAD_DIM)),
        "k_norm": gain(ks[5], (DEPTH, HEAD_DIM)),
        "hy_conv_w": nrm(ks[6], (DEPTH, SHORT_CONV, 3 * C), SHORT_CONV ** -0.5),
        "hy_conv_b": nrm(ks[7], (DEPTH, 3 * C), 0.02),
        "flt_w1": nrm(ks[8], (DEPTH, FILTER_EMB, FILTER_HIDDEN), FILTER_EMB ** -0.5),
        "flt_b1": nrm(ks[9], (DEPTH, FILTER_HIDDEN), 0.02),
        "flt_freq1": gain(ks[10], (DEPTH, FILTER_HIDDEN)),
        "flt_w2": nrm(ks[11], (DEPTH, FILTER_HIDDEN, FILTER_HIDDEN), FILTER_HIDDEN ** -0.5),
        "flt_b2": nrm(ks[12], (DEPTH, FILTER_HIDDEN), 0.02),
        "flt_freq2": gain(ks[13], (DEPTH, FILTER_HIDDEN)),
        "flt_w3": nrm(ks[14], (DEPTH, FILTER_HIDDEN, FILTER_OUT), FILTER_HIDDEN ** -0.5),
        "flt_b3": nrm(ks[15], (DEPTH, FILTER_OUT), 0.02),
        "hy_skip": nrm(ks[16], (DEPTH, HYENA_ORDER, C), 0.5),
        "attn_out_norm": gain(ks[17], (DEPTH, ATTN_WIDTH)),
        "hy_out_norm": gain(ks[18], (DEPTH, HYENA_WIDTH)),
        "w_out": nrm(ks[19], (DEPTH, D_MODEL, D_MODEL), D_MODEL ** -0.5),
        "ffn_norm": gain(ks[20], (DEPTH, D_MODEL)),
        "w_up": nrm(ks[21], (DEPTH, D_MODEL, D_FF), D_MODEL ** -0.5),
        "w_down": nrm(ks[22], (DEPTH, D_FF, D_MODEL), D_FF ** -0.5),
    }


def reference(x_prompt, x_sample, mix_norm, w_in, q_norm, k_norm, hy_conv_w, hy_conv_b,
              flt_w1, flt_b1, flt_freq1, flt_w2, flt_b2, flt_freq2, flt_w3, flt_b3,
              hy_skip, attn_out_norm, hy_out_norm, w_out, ffn_norm, w_up, w_down):
    y_prompt = trunk(x_prompt, mix_norm, w_in, q_norm, k_norm, hy_conv_w, hy_conv_b,
                     flt_w1, flt_b1, flt_freq1, flt_w2, flt_b2, flt_freq2, flt_w3, flt_b3,
                     hy_skip, attn_out_norm, hy_out_norm, w_out, ffn_norm, w_up, w_down)
    y_sample = trunk(x_sample, mix_norm, w_in, q_norm, k_norm, hy_conv_w, hy_conv_b,
                     flt_w1, flt_b1, flt_freq1, flt_w2, flt_b2, flt_freq2, flt_w3, flt_b3,
                     hy_skip, attn_out_norm, hy_out_norm, w_out, ffn_norm, w_up, w_down)
    return (y_prompt, y_sample)
```

```python
import functools
import math

import jax
import jax.numpy as jnp
import numpy as np
from jax import lax
from jax.experimental import pallas as pl
from jax.experimental.pallas import tpu as pltpu

F32 = jnp.float32
BF16 = jnp.bfloat16

D_MODEL = 1024
SEQ = 2048
ATTN_WIDTH = 512
HYENA_WIDTH = 512
HEAD_DIM = 64
N_HEADS = 8
DILATED_BRANCHES = ((128, 1), (512, 4), (2048, 16))
ROPE_THETA = 10000.0
HYENA_ORDER = 2
SHORT_CONV = 3
FILTER_EMB = 33
FILTER_HIDDEN = 64
DECAY_TARGET = 1e-2
FAST_DECAY_PCT = 0.3
SLOW_DECAY_PCT = 1.5
D_FF = 4 * D_MODEL
IN_WIDTH = 3 * ATTN_WIDTH + 3 * HYENA_WIDTH
EPS = 1e-6
NEG_INF = -1e30

LANES = 128
N_PAIRS = ATTN_WIDTH // LANES
Q_BLK = 128
ROW_TILE = 512
CONV_BLK = 512
N_CONV_BLK = SEQ // CONV_BLK
N_SPEC = 2 * N_CONV_BLK - 1
FREQ_CHUNK = 32
VMEM_LIMIT = 56 * 1024 * 1024


def _cparams(n_axes):
    return pltpu.CompilerParams(
        dimension_semantics=("arbitrary",) * n_axes, vmem_limit_bytes=VMEM_LIMIT)


def _const_spec(shape):
    nd = len(shape)
    return pl.BlockSpec(shape, lambda *_: (0,) * nd, pipeline_mode=pl.Buffered(1))


def _inproj_kernel(x_ref, g_ref, w_ref, qkv_ref, u_ref):
    x = x_ref[...]
    ms = jnp.mean(x * x, axis=-1, keepdims=True)
    xn = (x * lax.rsqrt(ms + EPS) * g_ref[...]).astype(BF16)
    for c in range(3):
        cols = slice(c * ATTN_WIDTH, (c + 1) * ATTN_WIDTH)
        proj = jnp.dot(xn, w_ref[:, cols], preferred_element_type=F32).astype(BF16)
        for p in range(N_PAIRS):
            qkv_ref[c * N_PAIRS + p] = proj[:, p * LANES:(p + 1) * LANES]
    a3 = 3 * ATTN_WIDTH
    for c in range(3):
        cols = slice(a3 + c * HYENA_WIDTH, a3 + (c + 1) * HYENA_WIDTH)
        u_ref[:, c * HYENA_WIDTH:(c + 1) * HYENA_WIDTH] = jnp.dot(
            xn, w_ref[:, cols], preferred_element_type=F32).astype(BF16)


def _inproj(x, g, w):
    m = x.shape[0]
    return pl.pallas_call(
        _inproj_kernel,
        grid=(m // ROW_TILE,),
        in_specs=[
            pl.BlockSpec((ROW_TILE, D_MODEL), lambda i: (i, 0)),
            _const_spec((1, D_MODEL)),
            _const_spec((D_MODEL, IN_WIDTH)),
        ],
        out_specs=[
            pl.BlockSpec((3 * N_PAIRS, ROW_TILE, LANES), lambda i: (0, i, 0)),
            pl.BlockSpec((ROW_TILE, 3 * HYENA_WIDTH), lambda i: (i, 0)),
        ],
        out_shape=[
            jax.ShapeDtypeStruct((3 * N_PAIRS, m, LANES), BF16),
            jax.ShapeDtypeStruct((m, 3 * HYENA_WIDTH), BF16),
        ],
        compiler_params=_cparams(1),
        name="inproj",
    )(x, g, w)


def _filt_mlp_kernel(feat_ref, w1_ref, b1_ref, f1_ref, w2_ref, b2_ref, f2_ref,
                     w3_ref, b3_ref, decay_ref, mf_ref, mb_ref, g_ref):
    hp = lax.Precision.HIGHEST
    h = jnp.dot(feat_ref[...], w1_ref[...], preferred_element_type=F32, precision=hp)
    h = jnp.sin(f1_ref[...] * (h + b1_ref[...]))
    h = jnp.dot(h, w2_ref[...], preferred_element_type=F32, precision=hp)
    h = jnp.sin(f2_ref[...] * (h + b2_ref[...]))
    h = jnp.dot(h, w3_ref[...], preferred_element_type=F32, precision=hp) + b3_ref[...]
    c = HYENA_WIDTH
    dec = decay_ref[...]
    mf = mf_ref[...]
    mb = mb_ref[...]
    for o in range(HYENA_ORDER):
        fwd = h[:, (2 * o) * c:(2 * o + 1) * c]
        bwd = h[:, (2 * o + 1) * c:(2 * o + 2) * c]
        g_ref[:, o * c:(o + 1) * c] = dec * (mf * fwd + mb * bwd)


def _filt_mlp(feat, w1, b1, f1, w2, b2, f2, w3, b3, decay, mf, mb):
    rows = feat.shape[0]
    emb = feat.shape[1]
    c = HYENA_WIDTH
    row = lambda i: (i, 0)
    return pl.pallas_call(
        _filt_mlp_kernel,
        grid=(rows // ROW_TILE,),
        in_specs=[
            pl.BlockSpec((ROW_TILE, emb), row),
            _const_spec(w1.shape), _const_spec(b1.shape), _const_spec(f1.shape),
            _const_spec(w2.shape), _const_spec(b2.shape), _const_spec(f2.shape),
            _const_spec(w3.shape), _const_spec(b3.shape),
            pl.BlockSpec((ROW_TILE, c), row),
            pl.BlockSpec((ROW_TILE, 1), row),
            pl.BlockSpec((ROW_TILE, 1), row),
        ],
        out_specs=pl.BlockSpec((ROW_TILE, HYENA_ORDER * c), row),
        out_shape=jax.ShapeDtypeStruct((rows, HYENA_ORDER * c), F32),
        compiler_params=_cparams(1),
        name="filt_mlp",
    )(feat, w1, b1, f1, w2, b2, f2, w3, b3, decay, mf, mb)


def _split_dot(a, b):
    a_hi = a.astype(BF16)
    a_lo = (a - a_hi.astype(F32)).astype(BF16)
    b_hi = b.astype(BF16)
    b_lo = (b - b_hi.astype(F32)).astype(BF16)
    d = functools.partial(jnp.dot, preferred_element_type=F32)
    return d(a_hi, b_hi) + (d(a_hi, b_lo) + d(a_lo, b_hi))


def _filt_spec_kernel(ga_ref, gb_ref, cs_ref, wk_ref, sg_ref, hc_ref, hs_ref, hn_ref):
    m = CONV_BLK
    ga = ga_ref[...]
    gb = gb_ref[...]
    cs = cs_ref[...]
    fa = _split_dot(cs, ga)
    fb = _split_dot(cs, gb)
    wk = wk_ref[...]
    sg = sg_ref[...]
    hc_ref[0] = wk * (fa[:m] + sg * fb[:m])
    hs_ref[0] = wk * (fa[m:] + sg * fb[m:])
    nyq = jnp.sum(sg * (ga + gb), axis=0, keepdims=True)
    hn_ref[0] = nyq * (1.0 / (2 * m))


def _filt_spec(g, cs, wk, sg):
    m = CONV_BLK
    c2 = g.shape[1]
    return pl.pallas_call(
        _filt_spec_kernel,
        grid=(N_SPEC,),
        in_specs=[
            pl.BlockSpec((m, c2), lambda d: (d + 1, 0)),
            pl.BlockSpec((m, c2), lambda d: (d, 0)),
            _const_spec((2 * m, m)),
            _const_spec((m, 1)),
            _const_spec((m, 1)),
        ],
        out_specs=[
            pl.BlockSpec((1, m, c2), lambda d: (d, 0, 0)),
            pl.BlockSpec((1, m, c2), lambda d: (d, 0, 0)),
            pl.BlockSpec((1, 1, c2), lambda d: (d, 0, 0)),
        ],
        out_shape=[
            jax.ShapeDtypeStruct((N_SPEC, m, c2), F32),
            jax.ShapeDtypeStruct((N_SPEC, m, c2), F32),
            jax.ShapeDtypeStruct((N_SPEC, 1, c2), F32),
        ],
        compiler_params=_cparams(1),
        name="filt_spec",
    )(g, g, cs, wk, sg)


def _attn_kernel(q_ref, k_ref, v_ref, gq_ref, gk_ref, cos_ref, sina_ref, sinb_ref,
                 ones_ref, o_ref, qs, ks, vs, acc_o, acc_m, acc_l):
    lane = lax.broadcasted_iota(jnp.int32, (1, LANES), 1)
    head0 = lane < HEAD_DIM

    def prep(src_ref, g_ref, dst, scale):
        ones = ones_ref[...]
        for c in range(SEQ // ROW_TILE):
            rows = pl.ds(c * ROW_TILE, ROW_TILE)
            x = src_ref[rows, :].astype(F32)
            x2 = x * x
            hi = x2.astype(BF16)
            lo = (x2 - hi.astype(F32)).astype(BF16)
            ssq = (jnp.dot(hi, ones, preferred_element_type=F32)
                   + jnp.dot(lo, ones, preferred_element_type=F32))
            xn = x * lax.rsqrt(ssq * (1.0 / HEAD_DIM) + EPS) * g_ref[...]
            y = (xn * cos_ref[rows, :]
                 + pltpu.roll(xn, LANES - HEAD_DIM // 2, 1) * sina_ref[rows, :]
                 + pltpu.roll(xn, HEAD_DIM // 2, 1) * sinb_ref[rows, :])
            dst[rows, :] = y * scale if scale != 1.0 else y

    prep(q_ref, gq_ref, qs, 1.0 / math.sqrt(HEAD_DIM))
    prep(k_ref, gk_ref, ks, 1.0)
    vs[...] = v_ref[...].astype(F32)

    def run_branch(branch, window, dil):
        length = SEQ // dil
        half = window // (2 * dil)
        n_blk = length // Q_BLK
        win = min(Q_BLK + 2 * half, length)
        rel0 = (lax.broadcasted_iota(jnp.int32, (Q_BLK, win), 0)
                - lax.broadcasted_iota(jnp.int32, (Q_BLK, win), 1))

        def rows_of(start, size):
            if dil == 1:
                return pl.ds(start, size)
            return pl.ds(start, size, stride=dil)

        def body(idx, carry):
            r = idx // n_blk
            n = idx % n_blk
            w0 = jnp.clip(Q_BLK * n - half, 0, length - win)
            delta = Q_BLK * n - w0
            qrows = rows_of(r + dil * Q_BLK * n, Q_BLK)
            krows = rows_of(r + dil * w0, win)
            qb = qs[qrows, :]
            kw = ks[krows, :].astype(BF16)
            vw = vs[krows, :].astype(BF16)
            bias = jnp.where(jnp.abs(rel0 + delta) <= half, 0.0, NEG_INF).astype(F32)
            outs, ms, ls = [], [], []
            for h in range(2):
                sel = head0 if h == 0 else jnp.logical_not(head0)
                qh = jnp.where(sel, qb, 0.0).astype(BF16)
                s = lax.dot_general(qh, kw, (((1,), (1,)), ((), ())),
                                    preferred_element_type=F32) + bias
                mx = jnp.max(s, axis=-1, keepdims=True)
                p = jnp.exp(s - mx)
                ls.append(jnp.sum(p, axis=-1, keepdims=True))
                ms.append(mx)
                outs.append(jnp.dot(p.astype(BF16), vw, preferred_element_type=F32))
            o_new = jnp.where(head0, outs[0], outs[1])
            m_new = jnp.where(head0, ms[0], ms[1])
            l_new = jnp.where(head0, ls[0], ls[1])
            if branch == 0:
                acc_o[qrows, :] = o_new
                acc_m[qrows, :] = m_new
                acc_l[qrows, :] = l_new
            else:
                m_old = acc_m[qrows, :]
                m_tot = jnp.maximum(m_old, m_new)
                a = jnp.exp(m_old - m_tot)
                b = jnp.exp(m_new - m_tot)
                acc_o[qrows, :] = acc_o[qrows, :] * a + o_new * b
                acc_l[qrows, :] = acc_l[qrows, :] * a + l_new * b
                acc_m[qrows, :] = m_tot
            return carry

        lax.fori_loop(0, dil * n_blk, body, 0)

    for branch, (window, dil) in enumerate(DILATED_BRANCHES):
        run_branch(branch, window, dil)

    o_ref[...] = acc_o[...] / acc_l[...]


def _attn(qkv, gq, gk, cos_t, sina_t, sinb_t, ones_bd, n_seq):
    m = qkv.shape[1]
    tab = _const_spec((SEQ, LANES))
    vec = _const_spec((1, LANES))
    return pl.pallas_call(
        _attn_kernel,
        grid=(n_seq, N_PAIRS),
        in_specs=[
            pl.BlockSpec((None, SEQ, LANES), lambda b, p: (p, b, 0)),
            pl.BlockSpec((None, SEQ, LANES), lambda b, p: (N_PAIRS + p, b, 0)),
            pl.BlockSpec((None, SEQ, LANES), lambda b, p: (2 * N_PAIRS + p, b, 0)),
            vec, vec, tab, tab, tab,
            _const_spec((LANES, LANES)),
        ],
        out_specs=pl.BlockSpec((SEQ, LANES), lambda b, p: (b, p)),
        out_shape=jax.ShapeDtypeStruct((m, ATTN_WIDTH), F32),
        scratch_shapes=[pltpu.VMEM((SEQ, LANES), F32)] * 6,
        compiler_params=_cparams(2),
        name="attn",
    )(qkv, qkv, qkv, gq, gk, cos_t, sina_t, sinb_t, ones_bd)


EDGE_ROWS = 16


def _short_conv_block(src_ref, blk, w_ref, b_ref):
    m = CONV_BLK
    u = src_ref[blk * m:(blk + 1) * m, :].astype(F32)
    row = lax.broadcasted_iota(jnp.int32, (m, 1), 0)
    if blk == 0:
        before = jnp.zeros((1, u.shape[1]), F32)
    else:
        before = src_ref[blk * m - EDGE_ROWS:blk * m, :].astype(F32)[EDGE_ROWS - 1:EDGE_ROWS, :]
    if blk == N_CONV_BLK - 1:
        after = jnp.zeros((1, u.shape[1]), F32)
    else:
        after = src_ref[(blk + 1) * m:(blk + 1) * m + EDGE_ROWS, :].astype(F32)[0:1, :]
    prev = jnp.where(row == 0, before, pltpu.roll(u, 1, 0))
    nxt = jnp.where(row == m - 1, after, pltpu.roll(u, m - 1, 0))
    return b_ref[...] + prev * w_ref[0:1, :] + u * w_ref[1:2, :] + nxt * w_ref[2:3, :]


def _hyena_kernel(z_ref, x_ref, wz_ref, bz_ref, wx_ref, bx_ref, skip_ref, gain_ref,
                  hc_ref, hs_ref, hn_ref, cs_ref, csr_ref, sg_ref, o_ref,
                  zc, spec_c, spec_s, ab, *, conv_z, final_norm):
    m = CONV_BLK
    nb = N_CONV_BLK
    sg = sg_ref[...]
    nyq = []
    for j in range(nb):
        if conv_z:
            zj = _short_conv_block(z_ref, j, wz_ref, bz_ref)
        else:
            zj = z_ref[j * m:(j + 1) * m, :].astype(F32)
        zc[j * m:(j + 1) * m, :] = zj
        zf = jnp.dot(cs_ref[...], zj.astype(BF16), preferred_element_type=F32)
        spec_c[j] = zf[:m]
        spec_s[j] = zf[m:]
        nyq.append(jnp.sum(sg * zj, axis=0, keepdims=True))

    for i in range(nb):
        def chunk(c, carry, i=i):
            rows = pl.ds(pl.multiple_of(c * FREQ_CHUNK, FREQ_CHUNK), FREQ_CHUNK)
            a = jnp.zeros((FREQ_CHUNK, HYENA_WIDTH), F32)
            b = jnp.zeros((FREQ_CHUNK, HYENA_WIDTH), F32)
            for j in range(nb):
                d = i - j + nb - 1
                hc = hc_ref[d, rows, :]
                hs = hs_ref[d, rows, :]
                sc = spec_c[j, rows, :]
                ss = spec_s[j, rows, :]
                a = a + (hc * sc - hs * ss)
                b = b + (hc * ss + hs * sc)
            ab[rows, :] = a.astype(BF16)
            ab[pl.ds(pl.multiple_of(m + c * FREQ_CHUNK, FREQ_CHUNK), FREQ_CHUNK), :] = b.astype(BF16)
            return carry

        lax.fori_loop(0, m // FREQ_CHUNK, chunk, 0)
        nq = jnp.zeros((1, HYENA_WIDTH), F32)
        for j in range(nb):
            nq = nq + hn_ref[i - j + nb - 1] * nyq[j]
        y = jnp.dot(csr_ref[...], ab[...], preferred_element_type=F32) + sg * nq
        blk = slice(i * m, (i + 1) * m)
        gate = _short_conv_block(x_ref, i, wx_ref, bx_ref)
        out = gate * (y + skip_ref[...] * zc[blk, :])
        if final_norm:
            ms = jnp.mean(out * out, axis=-1, keepdims=True)
            out = out * lax.rsqrt(ms + EPS) * gain_ref[...]
        o_ref[blk, :] = out.astype(o_ref.dtype)


def _hyena_order(zsrc, z_col, u, x_col, conv_w, conv_b, skip, gain, hc, hs, hn, order,
                 cs, csr, sg, n_seq, conv_z, final_norm, out_dtype):
    c = HYENA_WIDTH
    m = CONV_BLK
    msz = zsrc.shape[0]
    kern = functools.partial(_hyena_kernel, conv_z=conv_z, final_norm=final_norm)
    wz = conv_w[:, z_col * c:(z_col + 1) * c] if conv_z else conv_w[:, :c]
    bz = conv_b[:, z_col * c:(z_col + 1) * c] if conv_z else conv_b[:, :c]
    wx = conv_w[:, x_col * c:(x_col + 1) * c]
    bx = conv_b[:, x_col * c:(x_col + 1) * c]
    return pl.pallas_call(
        kern,
        grid=(n_seq,),
        in_specs=[
            pl.BlockSpec((SEQ, c), lambda b: (b, z_col)),
            pl.BlockSpec((SEQ, c), lambda b: (b, x_col)),
            _const_spec((SHORT_CONV, c)), _const_spec((1, c)),
            _const_spec((SHORT_CONV, c)), _const_spec((1, c)),
            _const_spec((1, c)), _const_spec((1, c)),
            pl.BlockSpec((N_SPEC, m, c), lambda b: (0, 0, order), pipeline_mode=pl.Buffered(1)),
            pl.BlockSpec((N_SPEC, m, c), lambda b: (0, 0, order), pipeline_mode=pl.Buffered(1)),
            pl.BlockSpec((N_SPEC, 1, c), lambda b: (0, 0, order), pipeline_mode=pl.Buffered(1)),
            _const_spec((2 * m, m)), _const_spec((m, 2 * m)), _const_spec((m, 1)),
        ],
        out_specs=pl.BlockSpec((SEQ, c), lambda b: (b, 0)),
        out_shape=jax.ShapeDtypeStruct((msz, c), out_dtype),
        scratch_shapes=[
            pltpu.VMEM((SEQ, c), F32),
            pltpu.VMEM((N_CONV_BLK, m, c), F32), pltpu.VMEM((N_CONV_BLK, m, c), F32),
            pltpu.VMEM((2 * m, c), BF16),
        ],
        compiler_params=_cparams(1),
        name=f"hyena{order}",
    )(zsrc, u, wz, bz, wx, bx, skip, gain, hc, hs, hn, cs, csr, sg)


def _outffn_kernel(x_ref, a_ref, hy_ref, ga_ref, wo_ref, gf_ref, wu_ref, wd_ref, o_ref):
    a = a_ref[...]
    ms = jnp.mean(a * a, axis=-1, keepdims=True)
    an = (a * lax.rsqrt(ms + EPS) * ga_ref[...]).astype(BF16)
    h = (x_ref[...]
         + jnp.dot(an, wo_ref[:ATTN_WIDTH, :], preferred_element_type=F32)
         + jnp.dot(hy_ref[...], wo_ref[ATTN_WIDTH:, :], preferred_element_type=F32))
    ms = jnp.mean(h * h, axis=-1, keepdims=True)
    hn = (h * lax.rsqrt(ms + EPS) * gf_ref[...]).astype(BF16)
    o_ref[...] = h
    for c in range(D_FF // D_MODEL):
        cols = slice(c * D_MODEL, (c + 1) * D_MODEL)
        ff = jnp.dot(hn, wu_ref[:, cols], preferred_element_type=F32)
        ff = jnp.square(jnp.maximum(ff, 0.0)).astype(BF16)
        o_ref[...] += jnp.dot(ff, wd_ref[cols, :], preferred_element_type=F32)


def _outffn(x, attn, hy, ga, wo, gf, wu, wd):
    m = x.shape[0]
    row = lambda i: (i, 0)
    return pl.pallas_call(
        _outffn_kernel,
        grid=(m // ROW_TILE,),
        in_specs=[
            pl.BlockSpec((ROW_TILE, D_MODEL), row),
            pl.BlockSpec((ROW_TILE, ATTN_WIDTH), row),
            pl.BlockSpec((ROW_TILE, HYENA_WIDTH), row),
            _const_spec((1, ATTN_WIDTH)),
            _const_spec((D_MODEL, D_MODEL)),
            _const_spec((1, D_MODEL)),
            _const_spec((D_MODEL, D_FF)),
            _const_spec((D_FF, D_MODEL)),
        ],
        out_specs=pl.BlockSpec((ROW_TILE, D_MODEL), row),
        out_shape=jax.ShapeDtypeStruct((m, D_MODEL), F32),
        compiler_params=_cparams(1),
        name="outffn",
    )(x, attn, hy, ga, wo, gf, wu, wd)


def _rope_tables():
    half = HEAD_DIM // 2
    inv_freq = ROPE_THETA ** (-jnp.arange(half, dtype=F32) / half)
    ang = jnp.arange(SEQ, dtype=F32)[:, None] * inv_freq[None, :]
    cos = jnp.cos(ang)
    sin = jnp.sin(ang)
    zero = jnp.zeros_like(sin)
    reps = LANES // HEAD_DIM
    cos_t = jnp.tile(jnp.concatenate([cos, cos], axis=1), (1, reps))
    sina_t = jnp.tile(jnp.concatenate([-sin, zero], axis=1), (1, reps))
    sinb_t = jnp.tile(jnp.concatenate([zero, sin], axis=1), (1, reps))
    return cos_t, sina_t, sinb_t


def _filter_tables():
    length = SEQ
    d = jnp.arange(-length, length, dtype=jnp.int32)
    pos = jnp.abs(d).astype(F32)
    t = pos / max(length - 1, 1)
    bands = (FILTER_EMB - 1) // 2
    fr = jnp.linspace(1e-4, bands - 1, bands, dtype=F32)
    ang = 2.0 * math.pi * pos[:, None] * fr[None, :] / length
    feat = jnp.concatenate([t[:, None], jnp.cos(ang), -jnp.sin(ang)], axis=-1)
    deltas = jnp.linspace(math.log(DECAY_TARGET) / FAST_DECAY_PCT,
                          math.log(DECAY_TARGET) / SLOW_DECAY_PCT, HYENA_WIDTH, dtype=F32)
    decay = jnp.exp(-t[:, None] * jnp.abs(deltas)[None, :])
    mf = (d >= 0).astype(F32)[:, None]
    mb = ((d <= 0) & (d > -length)).astype(F32)[:, None]
    return feat, decay, mf, mb


def _dft_tables():
    m = CONV_BLK
    k = jnp.arange(m, dtype=jnp.int32)
    ks = (k[:, None] * k[None, :]) % (2 * m)
    ang = ks.astype(F32) * (math.pi / m)
    cos = jnp.cos(ang)
    sin = jnp.sin(ang)
    cs = jnp.concatenate([cos, sin], axis=0)
    csr = jnp.concatenate([cos, sin], axis=1)
    wk = jnp.where(k == 0, 1.0 / (2 * m), 1.0 / m).astype(F32)[:, None]
    sg = jnp.where(k % 2 == 0, 1.0, -1.0).astype(F32)[:, None]
    return cs, csr, wk, sg


def _layer(x2d, n_seq, tabs, mix_norm, w_in, q_norm, k_norm, hy_conv_w, hy_conv_b, hy_skip,
           attn_out_norm, hy_out_norm, w_out, ffn_norm, w_up, w_down, hc, hs, hn):
    cos_t, sina_t, sinb_t, ones_bd, cs_bf, csr_bf, sg = tabs
    qkv, u = _inproj(x2d, mix_norm[None, :], w_in.astype(BF16))
    reps = LANES // HEAD_DIM
    gq = jnp.tile(q_norm, reps)[None, :]
    gk = jnp.tile(k_norm, reps)[None, :]
    attn = _attn(qkv, gq, gk, cos_t, sina_t, sinb_t, ones_bd, n_seq)
    cb = hy_conv_b[None, :]
    common = dict(cs=cs_bf, csr=csr_bf, sg=sg, n_seq=n_seq)
    z1 = _hyena_order(u, 0, u, 1, hy_conv_w, cb, hy_skip[0:1], hy_out_norm[None, :], hc, hs, hn, 0,
                      conv_z=True, final_norm=False, out_dtype=F32, **common)
    hy = _hyena_order(z1, 0, u, 2, hy_conv_w, cb, hy_skip[1:2], hy_out_norm[None, :], hc, hs, hn, 1,
                      conv_z=False, final_norm=True, out_dtype=BF16, **common)
    return _outffn(x2d, attn, hy, attn_out_norm[None, :], w_out.astype(BF16),
                   ffn_norm[None, :], w_up.astype(BF16), w_down.astype(BF16))


def kernel(x_prompt, x_sample, mix_norm, w_in, q_norm, k_norm, hy_conv_w, hy_conv_b, flt_w1, flt_b1, flt_freq1, flt_w2, flt_b2, flt_freq2, flt_w3, flt_b3, hy_skip, attn_out_norm, hy_out_norm, w_out, ffn_norm, w_up, w_down):
    depth = mix_norm.shape[0]
    cos_t, sina_t, sinb_t = _rope_tables()
    lane = np.arange(LANES)
    ones_bd = jnp.asarray((lane[:, None] // HEAD_DIM) == (lane[None, :] // HEAD_DIM), BF16)
    feat, decay, mf, mb = _filter_tables()
    cs, csr, wk, sg = _dft_tables()
    tabs = (cos_t, sina_t, sinb_t, ones_bd, cs.astype(BF16), csr.astype(BF16), sg)

    xs = [x_prompt.reshape(-1, D_MODEL), x_sample.reshape(-1, D_MODEL)]
    n_seqs = [x_prompt.shape[0], x_sample.shape[0]]
    for i in range(depth):
        g = _filt_mlp(feat, flt_w1[i], flt_b1[i][None, :], flt_freq1[i][None, :],
                      flt_w2[i], flt_b2[i][None, :], flt_freq2[i][None, :],
                      flt_w3[i], flt_b3[i][None, :], decay, mf, mb)
        hc, hs, hn = _filt_spec(g, cs, wk, sg)
        xs = [_layer(x, n, tabs, mix_norm[i], w_in[i], q_norm[i], k_norm[i], hy_conv_w[i],
                     hy_conv_b[i], hy_skip[i], attn_out_norm[i], hy_out_norm[i], w_out[i],
                     ffn_norm[i], w_up[i], w_down[i], hc, hs, hn)
              for x, n in zip(xs, n_seqs)]
    return (xs[0].reshape(x_prompt.shape), xs[1].reshape(x_sample.shape))
```

```python
import functools
import math

import jax
import jax.numpy as jnp
import numpy as np
from jax import lax
from jax.experimental import pallas as pl
from jax.experimental.pallas import tpu as pltpu

F32 = jnp.float32
BF16 = jnp.bfloat16

D_MODEL = 1024
SEQ = 2048
ATTN_WIDTH = 512
HYENA_WIDTH = 512
HEAD_DIM = 64
N_HEADS = 8
DILATED_BRANCHES = ((128, 1), (512, 4), (2048, 16))
ROPE_THETA = 10000.0
HYENA_ORDER = 2
SHORT_CONV = 3
FILTER_EMB = 33
FILTER_HIDDEN = 64
DECAY_TARGET = 1e-2
FAST_DECAY_PCT = 0.3
SLOW_DECAY_PCT = 1.5
D_FF = 4 * D_MODEL
IN_WIDTH = 3 * ATTN_WIDTH + 3 * HYENA_WIDTH
EPS = 1e-6
NEG_INF = -1e30
LOG2E = 1.4426950408889634

LANES = 128
SUBLANES = 8
N_PAIRS = ATTN_WIDTH // LANES
Q_BLK = 128
BLOCK_UNROLL = 16
DIL_STEP = 4
HALF_WIN = 64
KEY_WIN = Q_BLK + 2 * HALF_WIN
assert all(d2 == DIL_STEP * d1 for (_, d1), (_, d2) in zip(DILATED_BRANCHES, DILATED_BRANCHES[1:]))
assert DILATED_BRANCHES[0][1] == 1
ROW_TILE = 512
U_COLS = 256
HALO = 16
CONV_BLK = 512
N_CONV_BLK = SEQ // CONV_BLK
N_SPEC = 2 * N_CONV_BLK - 1
FREQ_CHUNK = 32
SPEC_UNROLL = 16
SPEC_DTYPE = BF16
VMEM_LIMIT = 56 * 1024 * 1024


def _cparams(n_axes):
    return pltpu.CompilerParams(
        dimension_semantics=("arbitrary",) * n_axes, vmem_limit_bytes=VMEM_LIMIT)


def _const_spec(shape):
    nd = len(shape)
    return pl.BlockSpec(shape, lambda *_: (0,) * nd, pipeline_mode=pl.Buffered(1))


def _inproj_kernel(x_ref, xb_ref, xa_ref, g_ref, w_ref, cw_ref, cb_ref, qkv_ref, u_ref):
    def norm(x):
        ms = jnp.mean(x * x, axis=-1, keepdims=True)
        return (x * lax.rsqrt(ms + EPS) * g_ref[...]).astype(BF16)

    xn = norm(x_ref[...])
    tiles_per_seq = SEQ // ROW_TILE
    pos = pl.program_id(0) % tiles_per_seq
    keep_before = (pos != 0).astype(F32)
    keep_after = (pos != tiles_per_seq - 1).astype(F32)
    xall = jnp.concatenate([xn, norm(xb_ref[...]), norm(xa_ref[...])], axis=0)
    row8 = lax.broadcasted_iota(jnp.int32, (SUBLANES, 1), 0)
    a3 = 3 * ATTN_WIDTH

    n_ucol = 3 * HYENA_WIDTH // U_COLS

    def project_u(c):
        cols = slice(a3 + c * U_COLS, a3 + (c + 1) * U_COLS)
        return jnp.dot(xall, w_ref[:, cols], preferred_element_type=F32)

    def short_conv(c, proj):
        ocols = slice(c * U_COLS, (c + 1) * U_COLS)
        u = proj[:ROW_TILE]
        before = proj[ROW_TILE + HALO - 1:ROW_TILE + HALO] * keep_before
        after = proj[ROW_TILE + HALO:ROW_TILE + HALO + 1] * keep_after
        prev = pltpu.roll(u, 1, 0)
        prev = jnp.concatenate([jnp.where(row8 == 0, before, prev[:SUBLANES]), prev[SUBLANES:]], axis=0)
        nxt = pltpu.roll(u, ROW_TILE - 1, 0)
        nxt = jnp.concatenate([nxt[:-SUBLANES], jnp.where(row8 == SUBLANES - 1, after, nxt[-SUBLANES:])], axis=0)
        u_ref[:, ocols] = (cb_ref[:, ocols] + prev * cw_ref[0:1, ocols] + u * cw_ref[1:2, ocols]
                           + nxt * cw_ref[2:3, ocols]).astype(BF16)

    def project_qkv(c):
        cols = slice(c * ATTN_WIDTH, (c + 1) * ATTN_WIDTH)
        proj = jnp.dot(xn, w_ref[:, cols], preferred_element_type=F32).astype(BF16)
        for p in range(N_PAIRS):
            qkv_ref[c * N_PAIRS + p] = proj[:, p * LANES:(p + 1) * LANES]

    pu = project_u(0)
    for c in range(1, n_ucol):
        nxt_pu = project_u(c)
        short_conv(c - 1, pu)
        pu = nxt_pu
    project_qkv(0)
    short_conv(n_ucol - 1, pu)
    project_qkv(1)
    project_qkv(2)


def _inproj(x, g, w, conv_w, conv_b):
    m = x.shape[0]
    halo_per_tile = ROW_TILE // HALO
    n_halo = m // HALO
    return pl.pallas_call(
        _inproj_kernel,
        grid=(m // ROW_TILE,),
        in_specs=[
            pl.BlockSpec((ROW_TILE, D_MODEL), lambda i: (i, 0)),
            pl.BlockSpec((HALO, D_MODEL), lambda i: (jnp.maximum(i * halo_per_tile - 1, 0), 0)),
            pl.BlockSpec((HALO, D_MODEL), lambda i: (jnp.minimum((i + 1) * halo_per_tile, n_halo - 1), 0)),
            _const_spec((1, D_MODEL)),
            _const_spec((D_MODEL, IN_WIDTH)),
            _const_spec((SHORT_CONV, 3 * HYENA_WIDTH)),
            _const_spec((1, 3 * HYENA_WIDTH)),
        ],
        out_specs=[
            pl.BlockSpec((3 * N_PAIRS, ROW_TILE, LANES), lambda i: (0, i, 0)),
            pl.BlockSpec((ROW_TILE, 3 * HYENA_WIDTH), lambda i: (i, 0)),
        ],
        out_shape=[
            jax.ShapeDtypeStruct((3 * N_PAIRS, m, LANES), BF16),
            jax.ShapeDtypeStruct((m, 3 * HYENA_WIDTH), BF16),
        ],
        compiler_params=_cparams(1),
        name="inproj",
    )(x, x, x, g, w, conv_w, conv_b)


def _filt_mlp_kernel(feat_ref, w1_ref, b1_ref, f1_ref, w2_ref, b2_ref, f2_ref,
                     w3_ref, b3_ref, decay_ref, mf_ref, mb_ref, skip_ref, g_ref):
    h = _split_dot(feat_ref[...], w1_ref[...])
    h = jnp.sin(f1_ref[...] * (h + b1_ref[...]))
    h = _split_dot(h, w2_ref[...])
    h = jnp.sin(f2_ref[...] * (h + b2_ref[...]))
    h = _split_dot(h, w3_ref[...]) + b3_ref[...]
    c = HYENA_WIDTH
    dec = decay_ref[...]
    mf = mf_ref[...]
    mb = mb_ref[...]
    for o in range(HYENA_ORDER):
        fwd = h[:, (2 * o) * c:(2 * o + 1) * c]
        bwd = h[:, (2 * o + 1) * c:(2 * o + 2) * c]
        g_ref[:, o * c:(o + 1) * c] = (dec * (mf * fwd + mb * bwd)
                                       + (mf * mb) * skip_ref[:, o * c:(o + 1) * c])


def _filt_mlp(feat, w1, b1, f1, w2, b2, f2, w3, b3, decay, mf, mb, skip):
    rows = feat.shape[0]
    emb = feat.shape[1]
    c = HYENA_WIDTH
    row = lambda i: (i, 0)
    return pl.pallas_call(
        _filt_mlp_kernel,
        grid=(rows // ROW_TILE,),
        in_specs=[
            pl.BlockSpec((ROW_TILE, emb), row),
            _const_spec(w1.shape), _const_spec(b1.shape), _const_spec(f1.shape),
            _const_spec(w2.shape), _const_spec(b2.shape), _const_spec(f2.shape),
            _const_spec(w3.shape), _const_spec(b3.shape),
            pl.BlockSpec((ROW_TILE, c), row),
            pl.BlockSpec((ROW_TILE, 1), row),
            pl.BlockSpec((ROW_TILE, 1), row),
            _const_spec(skip.shape),
        ],
        out_specs=pl.BlockSpec((ROW_TILE, HYENA_ORDER * c), row),
        out_shape=jax.ShapeDtypeStruct((rows, HYENA_ORDER * c), F32),
        compiler_params=_cparams(1),
        name="filt_mlp",
    )(feat, w1, b1, f1, w2, b2, f2, w3, b3, decay, mf, mb, skip)


def _split_dot(a, b):
    a_hi = a.astype(BF16)
    a_lo = (a - a_hi.astype(F32)).astype(BF16)
    b_hi = b.astype(BF16)
    b_lo = (b - b_hi.astype(F32)).astype(BF16)
    d = functools.partial(jnp.dot, preferred_element_type=F32)
    return d(a_hi, b_hi) + (d(a_hi, b_lo) + d(a_lo, b_hi))


def _filt_spec_kernel(g_ref, cs_ref, wk_ref, sg_ref, hc_ref, hs_ref, hn_ref, prev_f, prev_n):
    m = CONV_BLK
    j = pl.program_id(0)
    g = g_ref[...]
    f = _split_dot(cs_ref[...], g)
    sg = sg_ref[...]
    nyq = jnp.sum(sg * g, axis=0, keepdims=True)

    @pl.when(j == 0)
    def _first():
        hc_ref[...] = jnp.zeros_like(hc_ref)
        hs_ref[...] = jnp.zeros_like(hs_ref)
        hn_ref[...] = jnp.zeros_like(hn_ref)

    @pl.when(j > 0)
    def _pair():
        wk = wk_ref[...]
        fb = prev_f[...]
        hc_ref[0] = (wk * (f[:m] + sg * fb[:m])).astype(SPEC_DTYPE)
        hs_ref[0] = (wk * (f[m:] + sg * fb[m:])).astype(SPEC_DTYPE)
        hn_ref[0] = (nyq + prev_n[...]) * (1.0 / (2 * m))

    prev_f[...] = f
    prev_n[...] = nyq


def _filt_spec(g, cs, wk, sg):
    m = CONV_BLK
    c2 = g.shape[1]
    out_idx = lambda j: (jnp.maximum(j - 1, 0), 0, 0)
    return pl.pallas_call(
        _filt_spec_kernel,
        grid=(N_SPEC + 1,),
        in_specs=[
            pl.BlockSpec((m, c2), lambda j: (j, 0)),
            _const_spec((2 * m, m)),
            _const_spec((m, 1)),
            _const_spec((m, 1)),
        ],
        out_specs=[
            pl.BlockSpec((1, m, c2), out_idx),
            pl.BlockSpec((1, m, c2), out_idx),
            pl.BlockSpec((1, 1, c2), out_idx),
        ],
        out_shape=[
            jax.ShapeDtypeStruct((N_SPEC, m, c2), SPEC_DTYPE),
            jax.ShapeDtypeStruct((N_SPEC, m, c2), SPEC_DTYPE),
            jax.ShapeDtypeStruct((N_SPEC, 1, c2), F32),
        ],
        scratch_shapes=[pltpu.VMEM((2 * m, c2), F32), pltpu.VMEM((1, c2), F32)],
        compiler_params=_cparams(1),
        name="filt_spec",
    )(g, cs, wk, sg)


def _attn_kernel(q_ref, k_ref, v_ref, qtab_ref, ktab_ref, ones_ref, o_ref,
                 qs, ks, vs, acc_o, acc_m, acc_l, bias_s):
    lane = lax.broadcasted_iota(jnp.int32, (1, LANES), 1)
    head0 = lane < HEAD_DIM
    n_lvl = len(DILATED_BRANCHES)
    quarter = SEQ // DIL_STEP

    def prep(src_ref, tab_ref, dst_all):
        dst = dst_all.at[0]
        ones = ones_ref[...]
        for c in range(SEQ // ROW_TILE):
            rows = pl.ds(c * ROW_TILE, ROW_TILE)
            x = src_ref[rows, :].astype(F32)
            ms = jnp.dot((x * x).astype(BF16), ones, preferred_element_type=F32)
            y = (x * tab_ref[0, rows, :]
                 + pltpu.roll(x, LANES - HEAD_DIM // 2, 1) * tab_ref[1, rows, :]
                 + pltpu.roll(x, HEAD_DIM // 2, 1) * tab_ref[2, rows, :])
            dst[rows, :] = y * lax.rsqrt(ms + EPS)

    prep(q_ref, qtab_ref, qs)
    prep(k_ref, ktab_ref, ks)
    vs[0] = v_ref[...].astype(F32)

    for lvl in range(n_lvl - 1):
        for buf in (qs, ks, vs):
            for r in range(DIL_STEP):
                buf[lvl + 1, r * quarter:(r + 1) * quarter, :] = buf[lvl, pl.ds(r, quarter, stride=DIL_STEP), :]

    run_len = Q_BLK // DIL_STEP

    @pl.when((pl.program_id(0) == 0) & (pl.program_id(1) == 0))
    def _build_masks():
        row_i = lax.broadcasted_iota(jnp.int32, (Q_BLK, KEY_WIN), 0)
        col_i = lax.broadcasted_iota(jnp.int32, (Q_BLK, KEY_WIN), 1)
        row_perm = DIL_STEP * (row_i % run_len) + row_i // run_len
        for t in range(3):
            bias_s[t] = jnp.where(jnp.abs(row_i - col_i + t * HALF_WIN) <= HALF_WIN, 0.0, NEG_INF).astype(F32)
            bias_s[3 + t] = jnp.where(jnp.abs(row_perm - col_i + t * HALF_WIN) <= HALF_WIN,
                                      0.0, NEG_INF).astype(F32)

    def run_branch(lvl, first):
        window, dil = DILATED_BRANCHES[lvl]
        assert window // (2 * dil) == HALF_WIN
        length = SEQ // dil
        n_blk = length // Q_BLK
        win = min(KEY_WIN, length)
        ones_w = jnp.ones((win, LANES), BF16)
        group = BLOCK_UNROLL

        def acc_rows(chunk, n):
            if lvl == 2:
                return [pl.ds((chunk % DIL_STEP) * quarter + chunk // DIL_STEP, Q_BLK, stride=DIL_STEP)]
            if lvl == 1:
                return [pl.ds(pl.multiple_of(chunk * length + Q_BLK * n, Q_BLK), Q_BLK)]
            return [pl.ds(pl.multiple_of(r * quarter + run_len * n, run_len), run_len) for r in range(DIL_STEP)]

        def load_rows(ref, rows):
            return jnp.concatenate([ref[r, :] for r in rows], axis=0) if len(rows) > 1 else ref[rows[0], :]

        def store_rows(ref, rows, val):
            size = val.shape[0] // len(rows)
            for i, r in enumerate(rows):
                ref[r, :] = val[i * size:(i + 1) * size]

        def body(grp, carry):
            units = []
            for j in range(group):
                idx = grp * group + j
                chunk = idx // n_blk
                n = idx % n_blk
                w0 = jnp.clip(Q_BLK * n - HALF_WIN, 0, length - win)
                k0 = pl.multiple_of(chunk * length + w0, HALF_WIN)
                if n_blk == 1:
                    bias = bias_s[0, :, :win]
                else:
                    kind = jnp.where(n == 0, 0, jnp.where(n == n_blk - 1, 2, 1))
                    bias = bias_s[kind + (3 if lvl == 0 else 0)]
                rows = acc_rows(chunk, n)
                if lvl == 2:
                    qb = qs[2, pl.ds(pl.multiple_of(chunk * length, Q_BLK), Q_BLK), :]
                else:
                    qb = load_rows(qs.at[1], rows)
                units.append(dict(rows=rows, qb=qb, krows=pl.ds(k0, win), bias=bias))
            for u in units:
                qb = u["qb"]
                kw = ks[lvl, u["krows"], :].astype(BF16)
                u["s"] = []
                for h in range(2):
                    sel = head0 if h == 0 else jnp.logical_not(head0)
                    qh = jnp.where(sel, qb, 0.0).astype(BF16)
                    u["s"].append(lax.dot_general(qh, kw, (((1,), (1,)), ((), ())),
                                                  preferred_element_type=F32) + u["bias"])
            for u in units:
                u["m"] = [jnp.max(s, axis=-1, keepdims=True) for s in u["s"]]
            for u in units:
                u["p"] = [jnp.exp2(s - m).astype(BF16) for s, m in zip(u["s"], u["m"])]
            for u in units:
                vw = jnp.concatenate([vs[lvl, u["krows"], :].astype(BF16), ones_w], axis=1)
                u["r"] = [jnp.dot(p, vw, preferred_element_type=F32) for p in u["p"]]
            for u in units:
                rows = u["rows"]
                res, ms = u["r"], u["m"]
                o_new = jnp.where(head0, res[0][:, :LANES], res[1][:, :LANES])
                l_new = jnp.where(head0, res[0][:, LANES:], res[1][:, LANES:])
                m_new = jnp.where(head0, ms[0], ms[1])
                if first:
                    store_rows(acc_o, rows, o_new)
                    store_rows(acc_m, rows, m_new)
                    store_rows(acc_l, rows, l_new)
                else:
                    m_old = load_rows(acc_m, rows)
                    m_tot = jnp.maximum(m_old, m_new)
                    a = jnp.exp2(m_old - m_tot)
                    b = jnp.exp2(m_new - m_tot)
                    store_rows(acc_o, rows, load_rows(acc_o, rows) * a + o_new * b)
                    store_rows(acc_l, rows, load_rows(acc_l, rows) * a + l_new * b)
                    store_rows(acc_m, rows, m_tot)
            return carry

        lax.fori_loop(0, dil * n_blk // group, body, 0)

    assert n_lvl == 3
    for lvl in reversed(range(n_lvl)):
        run_branch(lvl, lvl == n_lvl - 1)

    for r in range(DIL_STEP):
        rows = slice(r * quarter, (r + 1) * quarter)
        o_ref[pl.ds(r, quarter, stride=DIL_STEP), :] = acc_o[rows, :] / acc_l[rows, :]


def _attn(qkv, qtab, ktab, ones_bd, n_seq):
    m = qkv.shape[1]
    tab = _const_spec((3, SEQ, LANES))
    return pl.pallas_call(
        _attn_kernel,
        grid=(n_seq, N_PAIRS),
        in_specs=[
            pl.BlockSpec((None, SEQ, LANES), lambda b, p: (p, b, 0)),
            pl.BlockSpec((None, SEQ, LANES), lambda b, p: (N_PAIRS + p, b, 0)),
            pl.BlockSpec((None, SEQ, LANES), lambda b, p: (2 * N_PAIRS + p, b, 0)),
            tab, tab,
            _const_spec((LANES, LANES)),
        ],
        out_specs=pl.BlockSpec((SEQ, LANES), lambda b, p: (b, p)),
        out_shape=jax.ShapeDtypeStruct((m, ATTN_WIDTH), F32),
        scratch_shapes=(
            [pltpu.VMEM((len(DILATED_BRANCHES), SEQ, LANES), F32)] * 3
            + [pltpu.VMEM((SEQ, LANES), F32)] * 3
            + [pltpu.VMEM((6, Q_BLK, KEY_WIN), F32)]),
        compiler_params=_cparams(2),
        name="attn",
    )(qkv, qkv, qkv, qtab, ktab, ones_bd)


def _hyena_kernel(z_ref, x_ref, gain_ref, hc_ref, hs_ref, hn_ref, cs_ref, csr_ref, sg_ref, o_ref,
                  spec_c, spec_s, ab, *, final_norm):
    m = CONV_BLK
    nb = N_CONV_BLK
    sg = sg_ref[...]
    nyq = []
    for j in range(nb):
        zj = z_ref[j * m:(j + 1) * m, :]
        zf = jnp.dot(cs_ref[...], zj, preferred_element_type=F32)
        spec_c[j] = zf[:m].astype(SPEC_DTYPE)
        spec_s[j] = zf[m:].astype(SPEC_DTYPE)
        nyq.append(jnp.sum(sg * zj.astype(F32), axis=0, keepdims=True))

    for i in range(nb):
        def chunk(c, carry, i=i):
            rows = pl.ds(pl.multiple_of(c * FREQ_CHUNK, FREQ_CHUNK), FREQ_CHUNK)
            a = jnp.zeros((FREQ_CHUNK, HYENA_WIDTH), SPEC_DTYPE)
            b = jnp.zeros((FREQ_CHUNK, HYENA_WIDTH), SPEC_DTYPE)
            for j in range(nb):
                d = i - j + nb - 1
                hc = hc_ref[d, rows, :]
                hs = hs_ref[d, rows, :]
                sc = spec_c[j, rows, :]
                ss = spec_s[j, rows, :]
                a = a + (hc * sc - hs * ss)
                b = b + (hc * ss + hs * sc)
            ab[rows, :] = a.astype(BF16)
            ab[pl.ds(pl.multiple_of(m + c * FREQ_CHUNK, FREQ_CHUNK), FREQ_CHUNK), :] = b.astype(BF16)
            return carry

        lax.fori_loop(0, m // FREQ_CHUNK, chunk, 0, unroll=SPEC_UNROLL)
        nq = jnp.zeros((1, HYENA_WIDTH), F32)
        for j in range(nb):
            nq = nq + hn_ref[i - j + nb - 1] * nyq[j]
        y = jnp.dot(csr_ref[...], ab[...], preferred_element_type=F32) + sg * nq
        blk = slice(i * m, (i + 1) * m)
        out = x_ref[blk, :].astype(F32) * y
        if final_norm:
            ms = jnp.mean(out * out, axis=-1, keepdims=True)
            out = out * lax.rsqrt(ms + EPS) * gain_ref[...]
        o_ref[blk, :] = out.astype(o_ref.dtype)


def _hyena_order(zsrc, z_col, u, x_col, gain, hc, hs, hn, order, cs, csr, sg, n_seq, final_norm):
    c = HYENA_WIDTH
    m = CONV_BLK
    msz = zsrc.shape[0]
    kern = functools.partial(_hyena_kernel, final_norm=final_norm)
    return pl.pallas_call(
        kern,
        grid=(n_seq,),
        in_specs=[
            pl.BlockSpec((SEQ, c), lambda b: (b, z_col)),
            pl.BlockSpec((SEQ, c), lambda b: (b, x_col)),
            _const_spec((1, c)),
            pl.BlockSpec((N_SPEC, m, c), lambda b: (0, 0, order), pipeline_mode=pl.Buffered(1)),
            pl.BlockSpec((N_SPEC, m, c), lambda b: (0, 0, order), pipeline_mode=pl.Buffered(1)),
            pl.BlockSpec((N_SPEC, 1, c), lambda b: (0, 0, order), pipeline_mode=pl.Buffered(1)),
            _const_spec((2 * m, m)), _const_spec((m, 2 * m)), _const_spec((m, 1)),
        ],
        out_specs=pl.BlockSpec((SEQ, c), lambda b: (b, 0)),
        out_shape=jax.ShapeDtypeStruct((msz, c), BF16),
        scratch_shapes=[
            pltpu.VMEM((N_CONV_BLK, m, c), SPEC_DTYPE), pltpu.VMEM((N_CONV_BLK, m, c), SPEC_DTYPE),
            pltpu.VMEM((2 * m, c), BF16),
        ],
        compiler_params=_cparams(1),
        name=f"hyena{order}",
    )(zsrc, u, gain, hc, hs, hn, cs, csr, sg)


def _outffn_kernel(x_ref, a_ref, hy_ref, ga_ref, wo_ref, gf_ref, wu_ref, wd_ref, o_ref):
    a = a_ref[...]
    ms = jnp.mean(a * a, axis=-1, keepdims=True)
    an = (a * lax.rsqrt(ms + EPS) * ga_ref[...]).astype(BF16)
    h = (x_ref[...]
         + jnp.dot(an, wo_ref[:ATTN_WIDTH, :], preferred_element_type=F32)
         + jnp.dot(hy_ref[...], wo_ref[ATTN_WIDTH:, :], preferred_element_type=F32))
    ms = jnp.mean(h * h, axis=-1, keepdims=True)
    hn = (h * lax.rsqrt(ms + EPS) * gf_ref[...]).astype(BF16)
    o_ref[...] = h
    for c in range(D_FF // D_MODEL):
        cols = slice(c * D_MODEL, (c + 1) * D_MODEL)
        ff = jnp.dot(hn, wu_ref[:, cols], preferred_element_type=F32)
        ff = jnp.square(jnp.maximum(ff, 0.0)).astype(BF16)
        o_ref[...] += jnp.dot(ff, wd_ref[cols, :], preferred_element_type=F32)


def _outffn(x, attn, hy, ga, wo, gf, wu, wd):
    m = x.shape[0]
    row = lambda i: (i, 0)
    return pl.pallas_call(
        _outffn_kernel,
        grid=(m // ROW_TILE,),
        in_specs=[
            pl.BlockSpec((ROW_TILE, D_MODEL), row),
            pl.BlockSpec((ROW_TILE, ATTN_WIDTH), row),
            pl.BlockSpec((ROW_TILE, HYENA_WIDTH), row),
            _const_spec((1, ATTN_WIDTH)),
            _const_spec((D_MODEL, D_MODEL)),
            _const_spec((1, D_MODEL)),
            _const_spec((D_MODEL, D_FF)),
            _const_spec((D_FF, D_MODEL)),
        ],
        out_specs=pl.BlockSpec((ROW_TILE, D_MODEL), row),
        out_shape=jax.ShapeDtypeStruct((m, D_MODEL), F32),
        compiler_params=_cparams(1),
        name="outffn",
    )(x, attn, hy, ga, wo, gf, wu, wd)


def _rope_tables():
    half = HEAD_DIM // 2
    inv_freq = ROPE_THETA ** (-jnp.arange(half, dtype=F32) / half)
    ang = jnp.arange(SEQ, dtype=F32)[:, None] * inv_freq[None, :]
    cos = jnp.cos(ang)
    sin = jnp.sin(ang)
    zero = jnp.zeros_like(sin)
    reps = LANES // HEAD_DIM
    cos_t = jnp.tile(jnp.concatenate([cos, cos], axis=1), (1, reps))
    sina_t = jnp.tile(jnp.concatenate([-sin, zero], axis=1), (1, reps))
    sinb_t = jnp.tile(jnp.concatenate([zero, sin], axis=1), (1, reps))
    return cos_t, sina_t, sinb_t


def _rope_gain_tables(rope, gain, scale):
    cos_t, sina_t, sinb_t = rope
    half = HEAD_DIM // 2
    g = jnp.tile(gain, LANES // HEAD_DIM)[None, :] * scale
    return jnp.stack([cos_t * g, sina_t * jnp.roll(g, -half, axis=1), sinb_t * jnp.roll(g, half, axis=1)])


def _filter_tables():
    length = SEQ
    d = jnp.arange(-length, length, dtype=jnp.int32)
    pos = jnp.abs(d).astype(F32)
    t = pos / max(length - 1, 1)
    bands = (FILTER_EMB - 1) // 2
    fr = jnp.linspace(1e-4, bands - 1, bands, dtype=F32)
    ang = 2.0 * math.pi * pos[:, None] * fr[None, :] / length
    feat = jnp.concatenate([t[:, None], jnp.cos(ang), -jnp.sin(ang)], axis=-1)
    deltas = jnp.linspace(math.log(DECAY_TARGET) / FAST_DECAY_PCT,
                          math.log(DECAY_TARGET) / SLOW_DECAY_PCT, HYENA_WIDTH, dtype=F32)
    decay = jnp.exp(-t[:, None] * jnp.abs(deltas)[None, :])
    mf = (d >= 0).astype(F32)[:, None]
    mb = ((d <= 0) & (d > -length)).astype(F32)[:, None]
    return feat, decay, mf, mb


def _dft_tables():
    m = CONV_BLK
    k = jnp.arange(m, dtype=jnp.int32)
    ks = (k[:, None] * k[None, :]) % (2 * m)
    ang = ks.astype(F32) * (math.pi / m)
    cos = jnp.cos(ang)
    sin = jnp.sin(ang)
    cs = jnp.concatenate([cos, sin], axis=0)
    csr = jnp.concatenate([cos, sin], axis=1)
    wk = jnp.where(k == 0, 1.0 / (2 * m), 1.0 / m).astype(F32)[:, None]
    sg = jnp.where(k % 2 == 0, 1.0, -1.0).astype(F32)[:, None]
    return cs, csr, wk, sg


def _layer(x2d, n_seq, tabs, mix_norm, w_in, q_norm, k_norm, hy_conv_w, hy_conv_b,
           attn_out_norm, hy_out_norm, w_out, ffn_norm, w_up, w_down, hc, hs, hn):
    rope, ones_bd, cs_bf, csr_bf, sg = tabs
    qkv, u = _inproj(x2d, mix_norm[None, :], w_in.astype(BF16), hy_conv_w, hy_conv_b[None, :])
    qtab = _rope_gain_tables(rope, q_norm, LOG2E / math.sqrt(HEAD_DIM))
    ktab = _rope_gain_tables(rope, k_norm, 1.0)
    attn = _attn(qkv, qtab, ktab, ones_bd, n_seq)
    common = dict(cs=cs_bf, csr=csr_bf, sg=sg, n_seq=n_seq)
    z1 = _hyena_order(u, 0, u, 1, hy_out_norm[None, :], hc, hs, hn, 0, final_norm=False, **common)
    hy = _hyena_order(z1, 0, u, 2, hy_out_norm[None, :], hc, hs, hn, 1, final_norm=True, **common)
    return _outffn(x2d, attn, hy, attn_out_norm[None, :], w_out.astype(BF16),
                   ffn_norm[None, :], w_up.astype(BF16), w_down.astype(BF16))


def kernel(x_prompt, x_sample, mix_norm, w_in, q_norm, k_norm, hy_conv_w, hy_conv_b, flt_w1, flt_b1, flt_freq1, flt_w2, flt_b2, flt_freq2, flt_w3, flt_b3, hy_skip, attn_out_norm, hy_out_norm, w_out, ffn_norm, w_up, w_down):
    depth = mix_norm.shape[0]
    lane = np.arange(LANES)
    same_head = (lane[:, None] // HEAD_DIM) == (lane[None, :] // HEAD_DIM)
    ones_bd = jnp.asarray(same_head / HEAD_DIM, BF16)
    feat, decay, mf, mb = _filter_tables()
    cs, csr, wk, sg = _dft_tables()
    tabs = (_rope_tables(), ones_bd, cs.astype(BF16), csr.astype(BF16), sg)

    xs = [x_prompt.reshape(-1, D_MODEL), x_sample.reshape(-1, D_MODEL)]
    n_seqs = [x_prompt.shape[0], x_sample.shape[0]]
    for i in range(depth):
        g = _filt_mlp(feat, flt_w1[i], flt_b1[i][None, :], flt_freq1[i][None, :],
                      flt_w2[i], flt_b2[i][None, :], flt_freq2[i][None, :],
                      flt_w3[i], flt_b3[i][None, :], decay, mf, mb, hy_skip[i].reshape(1, -1))
        hc, hs, hn = _filt_spec(g, cs, wk, sg)
        xs = [_layer(x, n, tabs, mix_norm[i], w_in[i], q_norm[i], k_norm[i], hy_conv_w[i],
                     hy_conv_b[i], attn_out_norm[i], hy_out_norm[i], w_out[i],
                     ffn_norm[i], w_up[i], w_down[i], hc, hs, hn)
              for x, n in zip(xs, n_seqs)]
    return (xs[0].reshape(x_prompt.shape), xs[1].reshape(x_sample.shape))
```
